```python
import math
import jax, jax.numpy as jnp
from jax import lax
import numpy as np

D_MODEL = 1024
BATCH = 16
SEQ = 4096
DEPTH = 2
DEC_BATCH = 4
DEC_SEQ = 4096
PAST_LEN = 128

GRID_W = 64
NA_HEADS = 8
NA_HEAD_DIM = 64
NA_WIN_ROWS = 8
NA_WIN_COLS = 16
DIFF_HEADS = 4
DIFF_QK_DIM = 64
DIFF_V_DIM = 128
T5_BUCKETS = 32
T5_MAX_DIST = 128
Q_BLOCK = 128
N_EXPERTS = 16
N_GROUPS = 4
EXPERTS_PER_GROUP = N_EXPERTS // N_GROUPS
TOP_K = 2
D_FF_EXPERT = 1024
EPS = 1e-6

NA_WIDTH = NA_HEADS * NA_HEAD_DIM
DIFF_QK_WIDTH = DIFF_HEADS * 2 * DIFF_QK_DIM
DIFF_V_WIDTH = DIFF_HEADS * DIFF_V_DIM
IN_SPLITS = (NA_WIDTH, NA_WIDTH, NA_WIDTH, DIFF_QK_WIDTH, DIFF_QK_WIDTH, DIFF_V_WIDTH, D_MODEL, D_MODEL)
IN_WIDTH = 3 * NA_WIDTH + 2 * DIFF_QK_WIDTH + DIFF_V_WIDTH + 2 * D_MODEL

kernel_name = "hybrid_na_diffattn_grouped_moe_encoder"


def rmsnorm(x, g):
    x32 = x.astype(jnp.float32)
    y = x32 * lax.rsqrt(jnp.mean(x32 * x32, axis=-1, keepdims=True) + EPS)
    return y.astype(x.dtype) * g


def t5_bucket(rel):
    half = T5_BUCKETS // 2
    max_exact = half // 2
    base = jnp.where(rel > 0, half, 0)
    n = jnp.abs(rel)
    n_f = jnp.maximum(n, 1).astype(jnp.float32)
    large = max_exact + (jnp.log(n_f / max_exact) / math.log(T5_MAX_DIST / max_exact)
                         * (half - max_exact)).astype(jnp.int32)
    large = jnp.minimum(large, half - 1)
    return base + jnp.where(n < max_exact, n, large)


def neighborhood_attention(q, k, v, rpb):
    B, S = q.shape[0], q.shape[1]
    rows = S // GRID_W
    kr = min(NA_WIN_ROWS, rows)
    kc = NA_WIN_COLS
    shp = (B, rows, GRID_W, NA_HEADS, NA_HEAD_DIM)
    qg, kg, vg = q.reshape(shp), k.reshape(shp), v.reshape(shp)
    cols = jnp.arange(GRID_W)
    col_start = jnp.clip(cols - kc // 2, 0, GRID_W - kc)
    col_idx = col_start[:, None] + jnp.arange(kc)[None, :]
    dc = col_idx - cols[:, None] + (NA_WIN_COLS - 1)
    scale = NA_HEAD_DIM ** -0.5

    def row_step(args):
        q_r, r = args
        r_start = jnp.clip(r - kr // 2, 0, rows - kr)
        k_rows = lax.dynamic_slice_in_dim(kg, r_start, kr, axis=1)
        v_rows = lax.dynamic_slice_in_dim(vg, r_start, kr, axis=1)
        k_win = k_rows[:, :, col_idx]
        v_win = v_rows[:, :, col_idx]
        dr = r_start + jnp.arange(kr) - r + (NA_WIN_ROWS - 1)
        bias = rpb[:, dr[None, :, None], dc[:, None, :]]
        s = (jnp.einsum('bqhd,brqchd->bhqrc', q_r, k_win).astype(jnp.float32) * scale
             + bias[None].astype(jnp.float32))
        p = jax.nn.softmax(s.reshape(B, NA_HEADS, GRID_W, kr * kc), axis=-1)
        p = p.reshape(s.shape).astype(v.dtype)
        return jnp.einsum('bhqrc,brqchd->bqhd', p, v_win)

    out = lax.map(row_step, (jnp.moveaxis(qg, 1, 0), jnp.arange(rows)))
    return jnp.moveaxis(out, 0, 1).reshape(B, S, NA_WIDTH)


def diff_attention(q, k, v, lam, t5_table):
    B, S = q.shape[0], q.shape[1]
    nblk = S // Q_BLOCK
    scale = DIFF_QK_DIM ** -0.5
    k1, k2 = k[:, :, :, 0], k[:, :, :, 1]
    qb = jnp.moveaxis(q.reshape(B, nblk, Q_BLOCK, DIFF_HEADS, 2, DIFF_QK_DIM), 1, 0)
    kpos = jnp.arange(S)

    def block_step(args):
        q_blk, start = args
        qpos = start + jnp.arange(Q_BLOCK)
        bias = t5_table[t5_bucket(kpos[None, :] - qpos[:, None])]
        bias = jnp.transpose(bias, (2, 0, 1))[None].astype(jnp.float32)
        s1 = jnp.einsum('bqhd,bkhd->bhqk', q_blk[:, :, :, 0], k1).astype(jnp.float32) * scale + bias
        s2 = jnp.einsum('bqhd,bkhd->bhqk', q_blk[:, :, :, 1], k2).astype(jnp.float32) * scale + bias
        a = jax.nn.softmax(s1, axis=-1) - lam * jax.nn.softmax(s2, axis=-1)
        return jnp.einsum('bhqk,bkhd->bqhd', a.astype(v.dtype), v)

    out = lax.map(block_step, (qb, jnp.arange(nblk) * Q_BLOCK))
    return jnp.moveaxis(out, 0, 1).reshape(B, S, DIFF_HEADS, DIFF_V_DIM)


def grouped_moe(h, router_w, router_b, w_gate, w_up, w_down):
    B, S, D = h.shape
    t = h.reshape(B * S, D)
    probs = jax.nn.softmax((t @ router_w).astype(jnp.float32), axis=-1)
    sel = probs + router_b.astype(jnp.float32)
    grp_score = lax.top_k(sel.reshape(-1, N_GROUPS, EXPERTS_PER_GROUP), 2)[0].sum(-1)
    _, gidx = lax.top_k(grp_score, 1)
    gmask = jax.nn.one_hot(gidx[:, 0], N_GROUPS, dtype=jnp.float32)
    emask = jnp.repeat(gmask, EXPERTS_PER_GROUP, axis=-1) > 0
    masked = jnp.where(emask, sel, -jnp.inf)
    _, eidx = lax.top_k(masked, TOP_K)
    w = jnp.take_along_axis(probs, eidx, axis=-1)
    w = w / jnp.sum(w, axis=-1, keepdims=True)
    gates = jnp.sum(jax.nn.one_hot(eidx, N_EXPERTS, dtype=jnp.float32) * w[..., None], axis=1)
    gates = gates.astype(t.dtype)
    y = jnp.zeros_like(t)
    for e in range(N_EXPERTS):
        he = jax.nn.silu(t @ w_gate[e]) * (t @ w_up[e])
        y = y + gates[:, e:e + 1] * (he @ w_down[e])
    return y.reshape(B, S, D)


def trunk(x, c, norm1_g, norm2_g, w_ada, b_ada, w_in, na_rpb, lambda_q1, lambda_k1, lambda_q2,
          lambda_k2, subln_g, t5_table, w_pa, w_pb, w_o, router_w, router_b, w_gate, w_up, w_down,
          final_g):
    B, S, D = x.shape
    split_points = np.cumsum(IN_SPLITS)[:-1].tolist()
    for l in range(DEPTH):
        mod = jax.nn.silu(c) @ w_ada[l] + b_ada[l]
        sh1, sc1, g1, sh2, sc2, g2 = [m[:, None, :] for m in jnp.split(mod, 6, axis=-1)]
        h = rmsnorm(x, norm1_g[l]) * (1 + sc1) + sh1
        u = h @ w_in[l]
        qa, ka, va, qb, kb, vb, ga, gb = jnp.split(u, split_points, axis=-1)
        na_shape = (B, S, NA_HEADS, NA_HEAD_DIM)
        oa = neighborhood_attention(qa.reshape(na_shape), ka.reshape(na_shape), va.reshape(na_shape), na_rpb[l])
        lambda_init = 0.8 - 0.6 * math.exp(-0.3 * l)
        lam = (jnp.exp(jnp.sum(lambda_q1[l].astype(jnp.float32) * lambda_k1[l].astype(jnp.float32)))
               - jnp.exp(jnp.sum(lambda_q2[l].astype(jnp.float32) * lambda_k2[l].astype(jnp.float32)))
               + lambda_init)
        qk_shape = (B, S, DIFF_HEADS, 2, DIFF_QK_DIM)
        ob = diff_attention(qb.reshape(qk_shape), kb.reshape(qk_shape),
                            vb.reshape(B, S, DIFF_HEADS, DIFF_V_DIM), lam, t5_table)
        ob = (rmsnorm(ob, subln_g[l]) * (1 - lambda_init)).reshape(B, S, DIFF_V_WIDTH)
        merged = jax.nn.sigmoid(ga) * (oa @ w_pa[l]) + jax.nn.sigmoid(gb) * (ob @ w_pb[l])
        x = x + g1 * (merged @ w_o[l])
        h2 = rmsnorm(x, norm2_g[l]) * (1 + sc2) + sh2
        x = x + g2 * grouped_moe(h2, router_w, router_b, w_gate[l], w_up[l], w_down[l])
    return rmsnorm(x, final_g)


def setup_inputs(seed: int = 0) -> dict:
    key = jax.random.key(seed)
    ks = jax.random.split(key, 32)
    D = D_MODEL

    def n(k, shape, s):
        return jax.random.normal(k, shape, jnp.float32) * s

    return {
        "x_prompt": n(ks[0], (BATCH, SEQ, D), 1.0),
        "x_sample": n(ks[1], (DEC_BATCH, DEC_SEQ, D), 1.0),
        "c_prompt": n(ks[2], (BATCH, D), 1.0),
        "c_sample": n(ks[3], (DEC_BATCH, D), 1.0),
        "norm1_g": 1.0 + n(ks[4], (DEPTH, D), 0.01),
        "norm2_g": 1.0 + n(ks[5], (DEPTH, D), 0.01),
        "w_ada": n(ks[6], (DEPTH, D, 6 * D), 0.3 * D ** -0.5),
        "b_ada": n(ks[7], (DEPTH, 6 * D), 0.01),
        "w_in": n(ks[8], (DEPTH, D, IN_WIDTH), D ** -0.5),
        "na_rpb": n(ks[9], (DEPTH, NA_HEADS, 2 * NA_WIN_ROWS - 1, 2 * NA_WIN_COLS - 1), 0.1),
        "lambda_q1": n(ks[10], (DEPTH, DIFF_QK_DIM), 0.1),
        "lambda_k1": n(ks[11], (DEPTH, DIFF_QK_DIM), 0.1),
        "lambda_q2": n(ks[12], (DEPTH, DIFF_QK_DIM), 0.1),
        "lambda_k2": n(ks[13], (DEPTH, DIFF_QK_DIM), 0.1),
        "subln_g": 1.0 + n(ks[14], (DEPTH, DIFF_V_DIM), 0.01),
        "t5_table": n(ks[15], (T5_BUCKETS, DIFF_HEADS), 0.1),
        "w_pa": n(ks[16], (DEPTH, NA_WIDTH, D), NA_WIDTH ** -0.5),
        "w_pb": n(ks[17], (DEPTH, DIFF_V_WIDTH, D), DIFF_V_WIDTH ** -0.5),
        "w_o": n(ks[18], (DEPTH, D, D), D ** -0.5),
        "router_w": n(ks[19], (D, N_EXPERTS), D ** -0.5),
        "router_b": n(ks[20], (N_EXPERTS,), 0.01),
        "w_gate": n(ks[21], (DEPTH, N_EXPERTS, D, D_FF_EXPERT), D ** -0.5),
        "w_up": n(ks[22], (DEPTH, N_EXPERTS, D, D_FF_EXPERT), D ** -0.5),
        "w_down": n(ks[23], (DEPTH, N_EXPERTS, D_FF_EXPERT, D), D_FF_EXPERT ** -0.5),
        "final_g": 1.0 + n(ks[24], (D,), 0.01),
    }


def reference(x_prompt, x_sample, c_prompt, c_sample, norm1_g, norm2_g, w_ada, b_ada, w_in, na_rpb,
              lambda_q1, lambda_k1, lambda_q2, lambda_k2, subln_g, t5_table, w_pa, w_pb, w_o,
              router_w, router_b, w_gate, w_up, w_down, final_g):
    y_prompt = trunk(x_prompt, c_prompt, norm1_g, norm2_g, w_ada, b_ada, w_in, na_rpb, lambda_q1,
                     lambda_k1, lambda_q2, lambda_k2, subln_g, t5_table, w_pa, w_pb, w_o, router_w,
                     router_b, w_gate, w_up, w_down, final_g)
    y_sample = trunk(x_sample, c_sample, norm1_g, norm2_g, w_ada, b_ada, w_in, na_rpb, lambda_q1,
                     lambda_k1, lambda_q2, lambda_k2, subln_g, t5_table, w_pa, w_pb, w_o, router_w,
                     router_b, w_gate, w_up, w_down, final_g)
    return (y_prompt, y_sample)
```

```python
import functools
import math

import jax
import jax.numpy as jnp
import numpy as np
from jax import lax
from jax.experimental import pallas as pl
from jax.experimental.pallas import tpu as pltpu

F32 = jnp.float32
BF16 = jnp.bfloat16
I32 = jnp.int32

D_MODEL = 1024
GRID_W = 64
NA_HEADS = 8
NA_HEAD_DIM = 64
NA_WIN_ROWS = 8
NA_WIN_COLS = 16
DIFF_HEADS = 4
DIFF_QK_DIM = 64
DIFF_V_DIM = 128
T5_BUCKETS = 32
T5_MAX_DIST = 128
N_EXPERTS = 16
N_GROUPS = 4
EXPERTS_PER_GROUP = N_EXPERTS // N_GROUPS
D_FF_EXPERT = 1024
EPS = 1e-6
N_MOD = 6

LANE = 128
SUBLANE = 8
VMEM_LIMIT_BYTES = 56 * 1024 * 1024

NA_PAIRS = NA_HEADS * NA_HEAD_DIM // LANE
G_QA, G_KA, G_VA = 0, NA_PAIRS, 2 * NA_PAIRS
G_QB = 3 * NA_PAIRS
G_KB = G_QB + DIFF_HEADS
G_VB = G_KB + DIFF_HEADS
N_GROUPS_ATT = G_VB + DIFF_HEADS
ATT_WIDTH = N_GROUPS_ATT * LANE
GATE_WIDTH = 2 * D_MODEL
IN_WIDTH = ATT_WIDTH + GATE_WIDTH

TM_PROJ = 512
NA_QROWS = 4
NA_KROWS = 12
NA_QB = NA_QROWS * GRID_W
NA_KB = NA_KROWS * GRID_W
DIFF_QB = 256
DIFF_KB = 256
T5_TILES = 5
TM_MOE = 512
TP_PLAN = 1024
TS_SCATTER = 512
TC_COMBINE = 256
NEG_BIAS = -1e30

_NT = (((1,), (1,)), ((), ()))


def _cparams(sem, vmem=VMEM_LIMIT_BYTES):
    return pltpu.CompilerParams(dimension_semantics=sem, vmem_limit_bytes=vmem)


def _rms(x):
    return x * lax.rsqrt(jnp.mean(x * x, axis=-1, keepdims=True) + EPS)


def _mod_kernel(c_ref, w_ref, b_ref, o_ref):
    c = c_ref[...]
    a = c * jax.nn.sigmoid(c)
    o_ref[0] = jnp.dot(a, w_ref[0], precision=lax.Precision.HIGHEST,
                       preferred_element_type=F32) + b_ref[0]


def _modulation(c_pad, w_ada, b_ada):
    depth, d, w6 = w_ada.shape
    bp = c_pad.shape[0]
    nblk = w6 // d
    return pl.pallas_call(
        _mod_kernel,
        out_shape=jax.ShapeDtypeStruct((depth, bp, w6), F32),
        grid=(depth, nblk),
        in_specs=[
            pl.BlockSpec((bp, d), lambda l, j: (0, 0)),
            pl.BlockSpec((1, d, d), lambda l, j: (l, 0, j)),
            pl.BlockSpec((1, 1, d), lambda l, j: (l, 0, j)),
        ],
        out_specs=pl.BlockSpec((1, bp, d), lambda l, j: (l, 0, j)),
        compiler_params=_cparams(("arbitrary", "arbitrary")),
        name="adaln_mod",
    )(c_pad, w_ada, b_ada.reshape(depth, 1, w6))


def _t5_large_thresholds():
    half = T5_BUCKETS // 2
    max_exact = half // 2
    n = np.arange(max_exact, 1 << 16, dtype=np.int64)
    nf = n.astype(np.float32)
    val = (np.log(nf / np.float32(max_exact)) / np.float32(math.log(T5_MAX_DIST / max_exact))
           * np.float32(half - max_exact)).astype(np.int32)
    large = np.minimum(max_exact + val, half - 1)
    assert np.all(np.diff(large) >= 0)
    thr = [int(n[np.argmax(large >= max_exact + k)]) for k in range(1, half - max_exact)]
    step = max_exact + sum((n >= t).astype(np.int64) for t in thr)
    assert np.array_equal(step, large)
    return thr


def _t5_kernel(tab_ref, o_ref, *, thresholds):
    h = pl.program_id(0)
    half = T5_BUCKETS // 2
    max_exact = half // 2
    row = lax.broadcasted_iota(I32, (DIFF_QB, DIFF_KB), 0)
    col = lax.broadcasted_iota(I32, (DIFF_QB, DIFF_KB), 1)
    for t in range(T5_TILES):
        rel = (t - T5_TILES // 2) * DIFF_KB + col - row
        n = jnp.abs(rel)
        large = jnp.full((DIFF_QB, DIFF_KB), max_exact, I32)
        for thr in thresholds:
            large = large + (n >= thr).astype(I32)
        bucket = jnp.where(rel > 0, half, 0) + jnp.where(n < max_exact, n, large)
        acc = jnp.zeros((DIFF_QB, DIFF_KB), F32)
        for b in range(T5_BUCKETS):
            acc = jnp.where(bucket == b, tab_ref[b, h], acc)
        o_ref[0, t] = acc


def _t5_tiles(t5_table):
    return pl.pallas_call(
        functools.partial(_t5_kernel, thresholds=_t5_large_thresholds()),
        out_shape=jax.ShapeDtypeStruct((DIFF_HEADS, T5_TILES, DIFF_QB, DIFF_KB), F32),
        grid=(DIFF_HEADS,),
        in_specs=[pl.BlockSpec(memory_space=pltpu.SMEM)],
        out_specs=pl.BlockSpec((1, T5_TILES, DIFF_QB, DIFF_KB), lambda h: (h, 0, 0, 0)),
        compiler_params=_cparams(("arbitrary",)),
        name="t5_bias_tiles",
    )(t5_table)


NA_CASES = 3
_RPB_ROWS = 2 * NA_WIN_ROWS - 1
_RPB_COLS = 2 * NA_WIN_COLS - 1


def _na_case_geometry(case, qr):
    if case == 0:
        return qr, 0
    if case == 1:
        return NA_QROWS + qr, qr
    return NA_KROWS - NA_QROWS + qr, NA_KROWS - NA_WIN_ROWS


def _nab_kernel(rpb_ref, o_ref):
    l = pl.program_id(0)
    h = pl.program_id(1)
    base = (l * NA_HEADS + h) * (_RPB_ROWS * _RPB_COLS)
    shape = (GRID_W, LANE)
    c = lax.broadcasted_iota(I32, shape, 0)
    lane = lax.broadcasted_iota(I32, shape, 1)
    kc = lane & (GRID_W - 1)
    right = lane >= GRID_W
    dc = kc - c + (NA_WIN_COLS - 1)
    cstart = jnp.clip(c - NA_WIN_COLS // 2, 0, GRID_W - NA_WIN_COLS)
    col_ok = (kc >= cstart) & (kc < cstart + NA_WIN_COLS)
    neg = jnp.full(shape, NEG_BIAS, F32)
    cache = {}

    def pair_tile(dr0):
        if dr0 not in cache:
            d_l = min(max(dr0, 0), _RPB_ROWS - 1)
            d_r = min(max(dr0 + 1, 0), _RPB_ROWS - 1)
            acc = jnp.zeros(shape, F32)
            for dcv in range(_RPB_COLS):
                s_l = rpb_ref[base + d_l * _RPB_COLS + dcv]
                s_r = rpb_ref[base + d_r * _RPB_COLS + dcv]
                acc = jnp.where(dc == dcv, jnp.where(right, s_r, s_l), acc)
            cache[dr0] = jnp.where(col_ok, acc, neg)
        return cache[dr0]

    for case in range(NA_CASES):
        for qr in range(NA_QROWS):
            off, first = _na_case_geometry(case, qr)
            for j in range(NA_KROWS // 2):
                kr0 = 2 * j
                ok_l = first <= kr0 < first + NA_WIN_ROWS
                ok_r = first <= kr0 + 1 < first + NA_WIN_ROWS
                dr0 = kr0 - off + (NA_WIN_ROWS - 1)
                if not ok_l and not ok_r:
                    blk = neg
                elif ok_l and ok_r:
                    blk = pair_tile(dr0)
                elif ok_l:
                    blk = jnp.where(right, neg, pair_tile(dr0))
                else:
                    blk = jnp.where(right, pair_tile(dr0), neg)
                o_ref[0, 0, case, qr * GRID_W:(qr + 1) * GRID_W, j * LANE:(j + 1) * LANE] = blk


def _na_bias_tiles(na_rpb):
    depth = na_rpb.shape[0]
    return pl.pallas_call(
        _nab_kernel,
        out_shape=jax.ShapeDtypeStruct((depth, NA_HEADS, NA_CASES, NA_QB, NA_KB), F32),
        grid=(depth, NA_HEADS),
        in_specs=[pl.BlockSpec(memory_space=pltpu.SMEM)],
        out_specs=pl.BlockSpec((1, 1, NA_CASES, NA_QB, NA_KB), lambda l, h: (l, h, 0, 0, 0)),
        compiler_params=_cparams(("arbitrary", "arbitrary")),
        name="na_bias_tiles",
    )(na_rpb.reshape(-1))


PROJ_CHUNK = 512


def _inproj_kernel(x_ref, mod_ref, g_ref, w_ref, u_ref, gate_ref):
    d = D_MODEL
    m = mod_ref[0, 0]
    shift, scale = m[:, 0:d], m[:, d:2 * d]
    h = (_rms(x_ref[...]) * g_ref[0]) * (1.0 + scale) + shift
    hb = h.astype(BF16)
    per = PROJ_CHUNK // LANE
    for j in range(ATT_WIDTH // PROJ_CHUNK):
        u = jnp.dot(hb, w_ref[0, :, j * PROJ_CHUNK:(j + 1) * PROJ_CHUNK],
                    preferred_element_type=F32).astype(BF16)
        for t in range(per):
            u_ref[0, j * per + t] = u[:, t * LANE:(t + 1) * LANE]
    for j in range(GATE_WIDTH // PROJ_CHUNK):
        lo = ATT_WIDTH + j * PROJ_CHUNK
        gate_ref[:, j * PROJ_CHUNK:(j + 1) * PROJ_CHUNK] = jnp.dot(
            hb, w_ref[0, :, lo:lo + PROJ_CHUNK], preferred_element_type=F32).astype(BF16)


def _in_projection(x, mod, norm_g, w_in, layer, bt, s):
    n = bt * s
    tps = s // TM_PROJ
    return pl.pallas_call(
        _inproj_kernel,
        out_shape=(jax.ShapeDtypeStruct((bt, N_GROUPS_ATT, s, LANE), BF16),
                   jax.ShapeDtypeStruct((n, GATE_WIDTH), BF16)),
        grid=(n // TM_PROJ,),
        in_specs=[
            pl.BlockSpec((TM_PROJ, D_MODEL), lambda i: (i, 0)),
            pl.BlockSpec((1, 1, 1, N_MOD * D_MODEL), lambda i: (layer, i // tps, 0, 0)),
            pl.BlockSpec((1, 1, D_MODEL), lambda i: (layer, 0, 0)),
            pl.BlockSpec((1, D_MODEL, IN_WIDTH), lambda i: (layer, 0, 0)),
        ],
        out_specs=(
            pl.BlockSpec((1, N_GROUPS_ATT, TM_PROJ, LANE), lambda i: (i // tps, 0, i % tps, 0)),
            pl.BlockSpec((TM_PROJ, GATE_WIDTH), lambda i: (i, 0)),
        ),
        compiler_params=_cparams(("arbitrary",)),
        name="in_projection",
    )(x, mod, norm_g, w_in)


def _na_kernel(q_ref, k_ref, v_ref, b_ref, o_ref, *, rows):
    i = pl.program_id(2)
    krow0 = jnp.clip(NA_QROWS * i - NA_QROWS, 0, rows - NA_KROWS)
    start = pl.multiple_of(krow0 * GRID_W, NA_QB)
    q = q_ref[0, 0]
    lo = lax.broadcasted_iota(I32, (NA_QB, LANE), 1) < NA_HEAD_DIM
    zero = jnp.zeros_like(q)
    scale = jnp.asarray(NA_HEAD_DIM ** -0.5, BF16)
    qq = jnp.concatenate([jnp.where(lo, q, zero), jnp.where(lo, zero, q)], axis=0) * scale
    kw = k_ref[0, 0, pl.ds(start, NA_KB), :]
    vw = v_ref[0, 0, pl.ds(start, NA_KB), :]
    s = lax.dot_general(qq, kw, _NT, preferred_element_type=F32)

    def softmax_parts(sh):
        m = jnp.max(sh, axis=-1, keepdims=True)
        p = jnp.exp(sh - m)
        return p, jnp.sum(p, axis=-1, keepdims=True)

    p0, l0 = softmax_parts(s[:NA_QB] + b_ref[0, 0, 0])
    p1, l1 = softmax_parts(s[NA_QB:] + b_ref[0, 1, 0])
    pp = jnp.concatenate([p0, p1], axis=0).astype(BF16)
    o = jnp.dot(pp, vw, preferred_element_type=F32)
    o_ref[0, 0] = jnp.where(lo, o[:NA_QB] / l0, o[NA_QB:] / l1).astype(BF16)


def _neighborhood_attention(u, nab, layer, bt, s):
    rows = s // GRID_W
    assert rows % NA_QROWS == 0 and rows >= NA_KROWS
    nblk = rows // NA_QROWS

    def case(i):
        return jnp.where(i == 0, 0, jnp.where(i == nblk - 1, 2, 1))

    return pl.pallas_call(
        functools.partial(_na_kernel, rows=rows),
        out_shape=jax.ShapeDtypeStruct((bt, NA_PAIRS, s, LANE), BF16),
        grid=(bt, NA_PAIRS, nblk),
        in_specs=[
            pl.BlockSpec((1, 1, NA_QB, LANE), lambda b, p, i: (b, G_QA + p, i, 0)),
            pl.BlockSpec((1, 1, s, LANE), lambda b, p, i: (b, G_KA + p, 0, 0)),
            pl.BlockSpec((1, 1, s, LANE), lambda b, p, i: (b, G_VA + p, 0, 0)),
            pl.BlockSpec((1, 2, 1, NA_QB, NA_KB), lambda b, p, i: (layer, p, case(i), 0, 0)),
        ],
        out_specs=pl.BlockSpec((1, 1, NA_QB, LANE), lambda b, p, i: (b, p, i, 0)),
        compiler_params=_cparams(("arbitrary", "arbitrary", "arbitrary")),
        name="neighborhood_attention",
    )(u, u, u, nab)


def _diff_kernel(q_ref, k_ref, v_ref, t5_ref, lq1_ref, lk1_ref, lq2_ref, lk2_ref, g_ref, o_ref, s_scr,
                 *, nchunk, lambda_init):
    i = pl.program_id(2)
    qb = DIFF_QB
    q = q_ref[0, 0]
    lo = lax.broadcasted_iota(I32, (qb, LANE), 1) < DIFF_QK_DIM
    zero = jnp.zeros_like(q)
    scale = jnp.asarray(DIFF_QK_DIM ** -0.5, BF16)
    qq = jnp.concatenate([jnp.where(lo, q, zero), jnp.where(lo, zero, q)], axis=0) * scale

    mrun = jnp.full((2 * qb, LANE), -jnp.inf, F32)
    for c in range(nchunk):
        kc = k_ref[0, 0, c * DIFF_KB:(c + 1) * DIFF_KB, :]
        s = lax.dot_general(qq, kc, _NT, preferred_element_type=F32)
        tile = jnp.clip(c - i, -(T5_TILES // 2), T5_TILES // 2) + T5_TILES // 2
        bias = t5_ref[0, tile]
        s = s + jnp.concatenate([bias, bias], axis=0)
        s_scr[c] = s
        for t in range(DIFF_KB // LANE):
            mrun = jnp.maximum(mrun, s[:, t * LANE:(t + 1) * LANE])
    m = jnp.max(mrun, axis=-1, keepdims=True)

    lrun = jnp.zeros((2 * qb, LANE), F32)
    acc = jnp.zeros((2 * qb, DIFF_V_DIM), F32)
    for c in range(nchunk):
        p = jnp.exp(s_scr[c] - m)
        for t in range(DIFF_KB // LANE):
            lrun = lrun + p[:, t * LANE:(t + 1) * LANE]
        acc = acc + jnp.dot(p.astype(BF16), v_ref[0, 0, c * DIFF_KB:(c + 1) * DIFF_KB, :],
                            preferred_element_type=F32)
    l = jnp.sum(lrun, axis=-1, keepdims=True)

    lam = (jnp.exp(jnp.sum(lq1_ref[0] * lk1_ref[0], axis=-1, keepdims=True))
           - jnp.exp(jnp.sum(lq2_ref[0] * lk2_ref[0], axis=-1, keepdims=True)) + lambda_init)
    o = acc[:qb] / l[:qb] - lam * (acc[qb:] / l[qb:])
    o_ref[0, 0] = ((_rms(o) * g_ref[0]) * (1.0 - lambda_init)).astype(BF16)


def _diff_attention(u, t5_tiles, lq1, lk1, lq2, lk2, subln_g, layer, bt, s):
    assert s % DIFF_QB == 0 and DIFF_QB == DIFF_KB
    nchunk = s // DIFF_KB
    lambda_init = 0.8 - 0.6 * math.exp(-0.3 * layer)
    vec = pl.BlockSpec((1, 1, DIFF_QK_DIM), lambda b, h, i: (layer, 0, 0))
    return pl.pallas_call(
        functools.partial(_diff_kernel, nchunk=nchunk, lambda_init=lambda_init),
        out_shape=jax.ShapeDtypeStruct((bt, DIFF_HEADS, s, LANE), BF16),
        grid=(bt, DIFF_HEADS, s // DIFF_QB),
        in_specs=[
            pl.BlockSpec((1, 1, DIFF_QB, LANE), lambda b, h, i: (b, G_QB + h, i, 0)),
            pl.BlockSpec((1, 1, s, LANE), lambda b, h, i: (b, G_KB + h, 0, 0)),
            pl.BlockSpec((1, 1, s, LANE), lambda b, h, i: (b, G_VB + h, 0, 0)),
            pl.BlockSpec((1, T5_TILES, DIFF_QB, DIFF_KB), lambda b, h, i: (h, 0, 0, 0)),
            vec, vec, vec, vec,
            pl.BlockSpec((1, 1, DIFF_V_DIM), lambda b, h, i: (layer, 0, 0)),
        ],
        out_specs=pl.BlockSpec((1, 1, DIFF_QB, LANE), lambda b, h, i: (b, h, i, 0)),
        scratch_shapes=[pltpu.VMEM((nchunk, 2 * DIFF_QB, DIFF_KB), F32)],
        compiler_params=_cparams(("arbitrary", "arbitrary", "arbitrary")),
        name="diff_attention",
    )(u, u, u, t5_tiles, lq1, lk1, lq2, lk2, subln_g)


def _route(logits_t, rb):
    tm = logits_t.shape[1]
    mx = jnp.max(logits_t, axis=0, keepdims=True)
    ex = jnp.exp(logits_t - mx)
    probs = ex / jnp.sum(ex, axis=0, keepdims=True)
    sel = probs + rb
    srow = [sel[e:e + 1, :] for e in range(N_EXPERTS)]
    prow = [probs[e:e + 1, :] for e in range(N_EXPERTS)]

    scores = []
    for g in range(N_GROUPS):
        a, b, c, d = srow[EXPERTS_PER_GROUP * g:EXPERTS_PER_GROUP * (g + 1)]
        hi1, lo1 = jnp.maximum(a, b), jnp.minimum(a, b)
        hi2, lo2 = jnp.maximum(c, d), jnp.minimum(c, d)
        scores.append(jnp.maximum(hi1, hi2) + jnp.maximum(jnp.minimum(hi1, hi2), jnp.maximum(lo1, lo2)))
    best = functools.reduce(jnp.maximum, scores)
    found = jnp.zeros((1, tm), jnp.bool_)
    chosen = []
    for g in range(N_GROUPS):
        hit = (scores[g] == best) & jnp.logical_not(found)
        chosen.append(hit)
        found = found | hit

    e_sel = [jnp.zeros((1, tm), I32), jnp.zeros((1, tm), I32)]
    p_sel = [jnp.zeros((1, tm), F32), jnp.zeros((1, tm), F32)]
    for g in range(N_GROUPS):
        for j in range(EXPERTS_PER_GROUP):
            e = EXPERTS_PER_GROUP * g + j
            rank = jnp.zeros((1, tm), I32)
            for j2 in range(EXPERTS_PER_GROUP):
                if j2 == j:
                    continue
                o = srow[EXPERTS_PER_GROUP * g + j2]
                ahead = (o > srow[e]) | ((o == srow[e]) & (j2 < j))
                rank = rank + ahead.astype(I32)
            for r in range(2):
                pick = chosen[g] & (rank == r)
                e_sel[r] = jnp.where(pick, e, e_sel[r])
                p_sel[r] = jnp.where(pick, prow[e], p_sel[r])
    wsum = p_sel[0] + p_sel[1]
    return e_sel, [p_sel[0] / wsum, p_sel[1] / wsum]


def _post_kernel(oa_ref, ob_ref, gate_ref, x_ref, mod_ref, wpa_ref, wpb_ref, wo_ref, g2_ref, rwt_ref, rb_ref,
                 xo_ref, h2_ref, e_ref, gcol_ref):
    d = D_MODEL
    tm = x_ref.shape[0]
    oa = jnp.concatenate([oa_ref[0, j] for j in range(NA_PAIRS)], axis=1)
    ob = jnp.concatenate([ob_ref[0, j] for j in range(DIFF_HEADS)], axis=1)
    pa = jnp.dot(oa, wpa_ref[0], preferred_element_type=F32)
    pb = jnp.dot(ob, wpb_ref[0], preferred_element_type=F32)
    ga = gate_ref[:, 0:d].astype(F32)
    gb = gate_ref[:, d:2 * d].astype(F32)
    merged = jax.nn.sigmoid(ga) * pa + jax.nn.sigmoid(gb) * pb
    m = mod_ref[0, 0]
    g1, shift2, scale2 = m[:, 2 * d:3 * d], m[:, 3 * d:4 * d], m[:, 4 * d:5 * d]
    xo = x_ref[...] + g1 * jnp.dot(merged.astype(BF16), wo_ref[0], preferred_element_type=F32)
    xo_ref[...] = xo
    h2 = (_rms(xo) * g2_ref[0]) * (1.0 + scale2) + shift2
    h2_ref[...] = h2
    logits_t = lax.dot_general(rwt_ref[...], h2.astype(BF16), _NT, preferred_element_type=F32)
    e_sel, w_sel = _route(logits_t, rb_ref[...])
    e_ref[...] = jnp.concatenate(e_sel, axis=0)
    wrows = jnp.concatenate(w_sel + [jnp.zeros((LANE - 2, tm), F32)], axis=0)
    gcol_ref[...] = wrows.T


def _post_attention(oa, ob, gates, x, mod, w_pa, w_pb, w_o, norm2_g, rwt, rb, layer, bt, s):
    n = bt * s
    tm = TM_PROJ
    tps = s // tm
    const3 = lambda i: (layer, 0, 0)
    return pl.pallas_call(
        _post_kernel,
        out_shape=(jax.ShapeDtypeStruct((n, D_MODEL), F32),
                   jax.ShapeDtypeStruct((n, D_MODEL), F32),
                   jax.ShapeDtypeStruct((2, n), I32),
                   jax.ShapeDtypeStruct((n, LANE), F32)),
        grid=(n // tm,),
        in_specs=[
            pl.BlockSpec((1, NA_PAIRS, tm, LANE), lambda i: (i // tps, 0, i % tps, 0)),
            pl.BlockSpec((1, DIFF_HEADS, tm, LANE), lambda i: (i // tps, 0, i % tps, 0)),
            pl.BlockSpec((tm, GATE_WIDTH), lambda i: (i, 0)),
            pl.BlockSpec((tm, D_MODEL), lambda i: (i, 0)),
            pl.BlockSpec((1, 1, 1, N_MOD * D_MODEL), lambda i: (layer, i // tps, 0, 0)),
            pl.BlockSpec((1, NA_HEADS * NA_HEAD_DIM, D_MODEL), const3),
            pl.BlockSpec((1, DIFF_HEADS * DIFF_V_DIM, D_MODEL), const3),
            pl.BlockSpec((1, D_MODEL, D_MODEL), const3),
            pl.BlockSpec((1, 1, D_MODEL), const3),
            pl.BlockSpec((N_EXPERTS, D_MODEL), lambda i: (0, 0)),
            pl.BlockSpec((N_EXPERTS, 1), lambda i: (0, 0)),
        ],
        out_specs=(
            pl.BlockSpec((tm, D_MODEL), lambda i: (i, 0)),
            pl.BlockSpec((tm, D_MODEL), lambda i: (i, 0)),
            pl.BlockSpec((2, tm), lambda i: (0, i)),
            pl.BlockSpec((tm, LANE), lambda i: (i, 0)),
        ),
        compiler_params=_cparams(("arbitrary",)),
        name="post_attention",
    )(oa, ob, gates, x, mod, w_pa, w_pb, w_o, norm2_g, rwt, rb)


def _plan_kernel(e_ref, pos_ref, te_ref, nt_ref, cnt_scr, run_scr, base_scr, tri_scr, *, tmax_pad):
    phase = pl.program_id(0)
    j = pl.program_id(1)
    tp = TP_PLAN
    e = e_ref[...]
    eid = lax.broadcasted_iota(I32, (N_EXPERTS, tp), 0)
    oh0 = eid == e[0:1, :]
    oh1 = eid == e[1:2, :]
    member = jnp.where(oh0 | oh1, 1.0, 0.0).astype(BF16)
    ones = jnp.ones((tp, LANE), BF16)
    shift = TM_MOE.bit_length() - 1

    @pl.when((phase == 0) & (j == 0))
    def _():
        cnt_scr[...] = jnp.zeros_like(cnt_scr)

    @pl.when(phase == 0)
    def _():
        cnt_scr[...] += jnp.dot(member, ones, preferred_element_type=F32)

    @pl.when((phase == 1) & (j == 0))
    def _():
        cnt = cnt_scr[...].astype(I32)
        padded = ((cnt + (TM_MOE - 1)) >> shift) << shift
        rowid = lax.broadcasted_iota(I32, (N_EXPERTS, LANE), 0)
        base = jnp.zeros((N_EXPERTS, LANE), I32)
        for e2 in range(N_EXPERTS - 1):
            base = base + jnp.where(rowid > e2, padded[e2:e2 + 1, :], 0)
        base_scr[...] = base.astype(F32)
        run_scr[...] = jnp.zeros_like(run_scr)
        end = base + padded
        tile_start = lax.broadcasted_iota(I32, (N_EXPERTS, tmax_pad), 1) * TM_MOE
        past = (tile_start >= jnp.broadcast_to(end[:, 0:1], (N_EXPERTS, tmax_pad))).astype(I32)
        te_ref[...] = jnp.minimum(jnp.sum(past, axis=0, keepdims=True), N_EXPERTS - 1)
        nt_ref[...] = end[N_EXPERTS - 1:N_EXPERTS, :] >> shift
        r = lax.broadcasted_iota(I32, (tp, tp), 0)
        c = lax.broadcasted_iota(I32, (tp, tp), 1)
        tri_scr[...] = jnp.where(r < c, 1.0, 0.0).astype(BF16)

    @pl.when(phase == 1)
    def _():
        before = jnp.dot(member, tri_scr[...], preferred_element_type=F32)
        rank = before + (base_scr[...][:, 0:1] + run_scr[...][:, 0:1])
        pos0 = jnp.sum(jnp.where(oh0, rank, 0.0), axis=0, keepdims=True)
        pos1 = jnp.sum(jnp.where(oh1, rank, 0.0), axis=0, keepdims=True)
        pos_ref[...] = jnp.concatenate([pos0, pos1], axis=0).astype(I32)
        run_scr[...] += jnp.dot(member, ones, preferred_element_type=F32)


def _max_tiles(n):
    return 2 * n // TM_MOE + N_EXPERTS


def _dispatch_plan(eidx, n):
    assert n % TP_PLAN == 0 and 2 * n + N_EXPERTS * TM_MOE < (1 << 24)
    tmax_pad = -(-_max_tiles(n) // LANE) * LANE
    return pl.pallas_call(
        functools.partial(_plan_kernel, tmax_pad=tmax_pad),
        out_shape=(jax.ShapeDtypeStruct((2, n), I32),
                   jax.ShapeDtypeStruct((1, tmax_pad), I32),
                   jax.ShapeDtypeStruct((1, LANE), I32)),
        grid=(2, n // TP_PLAN),
        in_specs=[pl.BlockSpec((2, TP_PLAN), lambda p, j: (0, j))],
        out_specs=(
            pl.BlockSpec((2, TP_PLAN), lambda p, j: (0, j * p)),
            pl.BlockSpec((1, tmax_pad), lambda p, j: (0, 0)),
            pl.BlockSpec((1, LANE), lambda p, j: (0, 0)),
        ),
        scratch_shapes=[pltpu.VMEM((N_EXPERTS, LANE), F32), pltpu.VMEM((N_EXPERTS, LANE), F32),
                        pltpu.VMEM((N_EXPERTS, LANE), F32), pltpu.VMEM((TP_PLAN, TP_PLAN), BF16)],
        compiler_params=_cparams(("arbitrary", "arbitrary")),
        name="dispatch_plan",
    )(eidx)


def _row_copy(src, dst, sem):
    return pltpu.make_async_copy(src, dst, sem)


def _scatter_kernel(pos_hbm, h_ref, xs_in, xs_ref, pos_smem, psem, sem):
    del xs_in
    i = pl.program_id(0)
    ts = TS_SCATTER
    rows = ts // LANE
    load = _row_copy(pos_hbm.at[:, pl.ds(i * rows, rows), :], pos_smem, psem)
    load.start()
    load.wait()

    def start(t, carry):
        for k in range(2):
            p = pos_smem[k, t >> 7, t & (LANE - 1)]
            _row_copy(h_ref.at[pl.ds(t, 1)], xs_ref.at[pl.ds(p, 1)], sem).start()
        return carry

    def wait(t, carry):
        for k in range(2):
            _row_copy(h_ref.at[pl.ds(0, 1)], xs_ref.at[pl.ds(0, 1)], sem).wait()
        return carry

    lax.fori_loop(0, ts, start, 0, unroll=8)
    lax.fori_loop(0, ts, wait, 0, unroll=8)


def _scatter_rows(pos3, h2, xs_zero):
    n = h2.shape[0]
    assert LANE == 128 and n % TS_SCATTER == 0
    return pl.pallas_call(
        _scatter_kernel,
        out_shape=jax.ShapeDtypeStruct(xs_zero.shape, xs_zero.dtype),
        grid=(n // TS_SCATTER,),
        in_specs=[
            pl.BlockSpec(memory_space=pl.ANY),
            pl.BlockSpec((TS_SCATTER, D_MODEL), lambda i: (i, 0)),
            pl.BlockSpec(memory_space=pl.ANY),
        ],
        out_specs=pl.BlockSpec(memory_space=pl.ANY),
        scratch_shapes=[pltpu.SMEM((2, TS_SCATTER // LANE, LANE), I32),
                        pltpu.SemaphoreType.DMA, pltpu.SemaphoreType.DMA],
        input_output_aliases={2: 0},
        compiler_params=_cparams(("arbitrary",)),
        name="moe_scatter",
    )(pos3, h2, xs_zero)


def _moe_kernel(te_ref, nt_ref, x_ref, wg_ref, wu_ref, wd_ref, y_ref):
    del te_ref
    used = pl.program_id(0) < nt_ref[0]

    @pl.when(used)
    def _():
        xb = x_ref[...].astype(BF16)
        g = jnp.dot(xb, wg_ref[0], preferred_element_type=F32)
        u = jnp.dot(xb, wu_ref[0], preferred_element_type=F32)
        h = (g * jax.nn.sigmoid(g)) * u
        y_ref[...] = jnp.dot(h.astype(BF16), wd_ref[0], preferred_element_type=F32)

    @pl.when(jnp.logical_not(used))
    def _():
        y_ref[...] = jnp.zeros_like(y_ref)


def _grouped_ffn(tile_expert, n_tiles, xs, w_gate, w_up, w_down, n):
    def in_tile(j, te, nt):
        return (jnp.minimum(j, nt[0] - 1), 0)

    def weight(j, te, nt):
        return (te[jnp.minimum(j, nt[0] - 1)], 0, 0)

    wspec = pl.BlockSpec((1, D_MODEL, D_FF_EXPERT), weight)
    grid_spec = pltpu.PrefetchScalarGridSpec(
        num_scalar_prefetch=2,
        grid=(_max_tiles(n),),
        in_specs=[pl.BlockSpec((TM_MOE, D_MODEL), in_tile), wspec, wspec,
                  pl.BlockSpec((1, D_FF_EXPERT, D_MODEL), weight)],
        out_specs=pl.BlockSpec((TM_MOE, D_MODEL), lambda j, te, nt: (j, 0)),
    )
    return pl.pallas_call(
        _moe_kernel,
        out_shape=jax.ShapeDtypeStruct(xs.shape, F32),
        grid_spec=grid_spec,
        compiler_params=_cparams(("arbitrary",)),
        name="moe_grouped_ffn",
    )(tile_expert, n_tiles, xs, w_gate, w_up, w_down)


def _combine_kernel(pos_hbm, gcol_ref, x_ref, mod_ref, fg_ref, ys_ref, o_ref, ybuf, pos_smem, psem, sem,
                    *, final):
    i = pl.program_id(0)
    tc = TC_COMBINE
    rows = tc // LANE
    load = _row_copy(pos_hbm.at[:, pl.ds(i * rows, rows), :], pos_smem, psem)
    load.start()
    load.wait()

    def start(t, carry):
        for k in range(2):
            p = pos_smem[k, t >> 7, t & (LANE - 1)]
            _row_copy(ys_ref.at[pl.ds(p, 1)], ybuf.at[k, pl.ds(t, 1)], sem).start()
        return carry

    def wait(t, carry):
        for k in range(2):
            _row_copy(ys_ref.at[pl.ds(0, 1)], ybuf.at[k, pl.ds(0, 1)], sem).wait()
        return carry

    lax.fori_loop(0, tc, start, 0, unroll=8)
    lax.fori_loop(0, tc, wait, 0, unroll=8)

    d = D_MODEL
    y = gcol_ref[:, 0:1] * ybuf[0] + gcol_ref[:, 1:2] * ybuf[1]
    g2 = mod_ref[0, 0][:, 5 * d:6 * d]
    xn = x_ref[...] + g2 * y
    if final:
        xn = _rms(xn) * fg_ref[...]
    o_ref[...] = xn


def _combine(pos3, gcol, x, mod, final_g, ys, layer, bt, s, final):
    n = bt * s
    tc = TC_COMBINE
    tps = s // tc
    return pl.pallas_call(
        functools.partial(_combine_kernel, final=final),
        out_shape=jax.ShapeDtypeStruct((n, D_MODEL), F32),
        grid=(n // tc,),
        in_specs=[
            pl.BlockSpec(memory_space=pl.ANY),
            pl.BlockSpec((tc, LANE), lambda i: (i, 0)),
            pl.BlockSpec((tc, D_MODEL), lambda i: (i, 0)),
            pl.BlockSpec((1, 1, 1, N_MOD * D_MODEL), lambda i: (layer, i // tps, 0, 0)),
            pl.BlockSpec((1, D_MODEL), lambda i: (0, 0)),
            pl.BlockSpec(memory_space=pl.ANY),
        ],
        out_specs=pl.BlockSpec((tc, D_MODEL), lambda i: (i, 0)),
        scratch_shapes=[pltpu.VMEM((2, tc, D_MODEL), F32),
                        pltpu.SMEM((2, tc // LANE, LANE), I32),
                        pltpu.SemaphoreType.DMA, pltpu.SemaphoreType.DMA],
        compiler_params=_cparams(("arbitrary",)),
        name="moe_combine",
    )(pos3, gcol, x, mod, final_g, ys)


def _trunk(x, c, norm1_g, norm2_g, w_ada, b_ada, w_in, na_rpb, lambda_q1, lambda_k1, lambda_q2, lambda_k2,
           subln_g, t5_table, w_pa, w_pb, w_o, router_w, router_b, w_gate, w_up, w_down, final_g):
    bt, s, d = x.shape
    assert d == D_MODEL and s % TM_PROJ == 0 and s % GRID_W == 0
    depth = w_in.shape[0]
    n = bt * s

    bpad = -(-bt // SUBLANE) * SUBLANE
    c_pad = jnp.pad(c, ((0, bpad - bt), (0, 0)))
    mod = _modulation(c_pad, w_ada, b_ada).reshape(depth, bpad, 1, N_MOD * d)
    t5_tiles = _t5_tiles(t5_table)
    nab = _na_bias_tiles(na_rpb)

    w_in_b = w_in.astype(BF16)
    w_pa_b, w_pb_b, w_o_b = w_pa.astype(BF16), w_pb.astype(BF16), w_o.astype(BF16)
    w_gate_b, w_up_b, w_down_b = w_gate.astype(BF16), w_up.astype(BF16), w_down.astype(BF16)
    rwt = router_w.T.astype(BF16)
    rb = router_b.reshape(N_EXPERTS, 1)
    g1 = norm1_g.reshape(depth, 1, d)
    g2 = norm2_g.reshape(depth, 1, d)
    sg = subln_g.reshape(depth, 1, DIFF_V_DIM)
    lvec = [v.reshape(depth, 1, DIFF_QK_DIM) for v in (lambda_q1, lambda_k1, lambda_q2, lambda_k2)]
    fg = final_g.reshape(1, d)
    sorted_rows = _max_tiles(n) * TM_MOE

    xf = x.reshape(n, d)
    for l in range(depth):
        u, gates = _in_projection(xf, mod, g1, w_in_b, l, bt, s)
        oa = _neighborhood_attention(u, nab, l, bt, s)
        ob = _diff_attention(u, t5_tiles, *lvec, sg, l, bt, s)
        xf, h2, eidx, gcol = _post_attention(oa, ob, gates, xf, mod, w_pa_b, w_pb_b, w_o_b, g2, rwt, rb, l, bt, s)
        pos, tile_expert, n_tiles = _dispatch_plan(eidx, n)
        pos3 = pos.reshape(2, n // LANE, LANE)
        xs = _scatter_rows(pos3, h2, jnp.zeros((sorted_rows, d), F32))
        ys = _grouped_ffn(tile_expert[0, :_max_tiles(n)], n_tiles[0, :1], xs,
                          w_gate_b[l], w_up_b[l], w_down_b[l], n)
        xf = _combine(pos3, gcol, xf, mod, fg, ys, l, bt, s, final=(l == depth - 1))
    return xf.reshape(bt, s, d)


def kernel(x_prompt, x_sample, c_prompt, c_sample, norm1_g, norm2_g, w_ada, b_ada, w_in, na_rpb, lambda_q1,
           lambda_k1, lambda_q2, lambda_k2, subln_g, t5_table, w_pa, w_pb, w_o, router_w, router_b, w_gate,
           w_up, w_down, final_g):
    assert x_prompt.shape[1:] == x_sample.shape[1:]
    bp = x_prompt.shape[0]
    x = jnp.concatenate([x_prompt, x_sample], axis=0)
    c = jnp.concatenate([c_prompt, c_sample], axis=0)
    y = _trunk(x, c, norm1_g, norm2_g, w_ada, b_ada, w_in, na_rpb, lambda_q1, lambda_k1, lambda_q2, lambda_k2,
               subln_g, t5_table, w_pa, w_pb, w_o, router_w, router_b, w_gate, w_up, w_down, final_g)
    return (y[:bp], y[bp:])
```
